```python
import jax, jax.numpy as jnp
from jax import lax
import numpy as np

D_MODEL = 1024
BATCH = 2
SEQ = 8192
DEPTH = 2

GRID_W = 64
CTX_LEN = 256
D_MIX = D_MODEL
W_F = D_MIX // 4
W_ATT = D_MIX // 4
W_CONV = D_MIX // 4
W_POOL = D_MIX // 4
HEAD_DIM = 64
N_HEADS = W_ATT // HEAD_DIM
N_KV_HEADS = N_HEADS // 2
GQA = N_HEADS // N_KV_HEADS
KV_DIM = N_KV_HEADS * HEAD_DIM
ATT_SCALE = HEAD_DIM ** -0.5
ROPE_FREQS = HEAD_DIM // 4
ROPE_THETA = 10000.0
Q_BLOCK = 128
F_HEADS = 4
F_DIM = W_F // F_HEADS
CONV_W = 3
POOL_WINDOWS = (2, 4, 8, 16)
POOL_DIM = W_POOL // len(POOL_WINDOWS)
EPS = 1e-6
SPLIT_WIDTHS = (W_ATT, KV_DIM, KV_DIM, W_ATT, W_F, W_F, W_CONV, W_CONV, W_CONV, W_CONV, W_POOL, W_POOL)
D_IN = sum(SPLIT_WIDTHS)

kernel_name = "hymba_style_parallel_hybrid_dit"


def _split_cols(p):
    return jnp.split(p, np.cumsum(SPLIT_WIDTHS)[:-1].tolist(), axis=-1)


def _rmsnorm(x, g):
    xf = x.astype(jnp.float32)
    y = xf * lax.rsqrt(jnp.mean(xf * xf, axis=-1, keepdims=True) + EPS)
    return (y * g.astype(jnp.float32)).astype(x.dtype)


def _axial_angles(n):
    n_rows = n // GRID_W
    row = jnp.repeat(jnp.arange(n_rows), GRID_W).astype(jnp.float32)
    col = jnp.tile(jnp.arange(GRID_W), n_rows).astype(jnp.float32)
    inv = ROPE_THETA ** (-jnp.arange(ROPE_FREQS, dtype=jnp.float32) / ROPE_FREQS)
    return row[:, None] * inv, col[:, None] * inv


def _rotate(x, ang):
    f = ang.shape[-1]
    cos = jnp.cos(ang)[None, :, None, :].astype(x.dtype)
    sin = jnp.sin(ang)[None, :, None, :].astype(x.dtype)
    x1, x2 = x[..., :f], x[..., f:]
    return jnp.concatenate([x1 * cos - x2 * sin, x2 * cos + x1 * sin], axis=-1)


def _axial_rope(x, ang_row, ang_col):
    half = HEAD_DIM // 2
    return jnp.concatenate([_rotate(x[..., :half], ang_row), _rotate(x[..., half:], ang_col)], axis=-1)


def _latent_attention(q, k, v, kc, vc):
    b, s = q.shape[0], q.shape[1]
    nblk = s // Q_BLOCK
    k_all = jnp.concatenate([kc, k], axis=1)
    v_all = jnp.concatenate([vc, v], axis=1)
    qb = q.reshape(b, nblk, Q_BLOCK, N_KV_HEADS, GQA, HEAD_DIM).transpose(1, 0, 2, 3, 4, 5)

    def one_block(qi):
        sc = jnp.einsum('bqkgd,bskd->bkgqs', qi, k_all, preferred_element_type=jnp.float32)
        p = jax.nn.softmax(sc, axis=-1).astype(v_all.dtype)
        return jnp.einsum('bkgqs,bskd->bqkgd', p, v_all)

    o = lax.map(one_block, qb)
    return o.transpose(1, 0, 2, 3, 4, 5).reshape(b, s, W_ATT)


def _context_attention(qc, kc, vc):
    b, n = qc.shape[0], qc.shape[1]
    qg = qc.reshape(b, n, N_KV_HEADS, GQA, HEAD_DIM)
    sc = jnp.einsum('bqkgd,bskd->bkgqs', qg, kc, preferred_element_type=jnp.float32)
    p = jax.nn.softmax(sc, axis=-1).astype(vc.dtype)
    return jnp.einsum('bkgqs,bskd->bqkgd', p, vc).reshape(b, n, W_ATT)


def _fourier_branch(u, z, w_fourier):
    b, n, _ = u.shape
    uf = u.astype(jnp.float32).reshape(b, n, F_HEADS, F_DIM)
    y = jnp.fft.fftn(uf, axes=(1, 3), norm='ortho').real.reshape(b, n, W_F).astype(u.dtype)
    return jax.nn.silu(z) * (y @ w_fourier)


def _conv_branch(bg, cg, hv, z, conv_w, conv_b):
    t = cg * hv
    tp = jnp.pad(t, ((0, 0), (1, 1), (0, 0)))
    y = tp[:, :-2] * conv_w[0] + tp[:, 1:-1] * conv_w[1] + tp[:, 2:] * conv_w[2] + conv_b
    return jax.nn.silu(z) * (bg * y)


def _pool_branch(u, z, pool_w, pool_scale):
    b, n, _ = u.shape
    cs = jnp.cumsum(u.astype(jnp.float32), axis=1)
    cs = jnp.concatenate([jnp.zeros((b, 1, W_POOL), jnp.float32), cs], axis=1)
    pos = jnp.arange(n)
    outs = []
    for i, w in enumerate(POOL_WINDOWS):
        left = w // 2
        right = w - 1 - left
        lo = jnp.clip(pos - left, 0, n)
        hi = jnp.clip(pos + right + 1, 0, n)
        sl = slice(i * POOL_DIM, (i + 1) * POOL_DIM)
        mean = (cs[:, hi, sl] - cs[:, lo, sl]) / (hi - lo).astype(jnp.float32)[None, :, None]
        d = (mean - u[..., sl].astype(jnp.float32)).astype(u.dtype)
        outs.append(d @ pool_w[i])
    y = jnp.concatenate(outs, axis=-1) * pool_scale
    return jax.nn.silu(z) * y


def _local_branches(parts, w_fourier, conv_w, conv_b, pool_w, pool_scale):
    _, _, _, _, u_f, z_f, b_c, c_c, h_c, z_c, u_p, z_p = parts
    o_f = _fourier_branch(u_f, z_f, w_fourier)
    o_c = _conv_branch(b_c, c_c, h_c, z_c, conv_w, conv_b)
    o_p = _pool_branch(u_p, z_p, pool_w, pool_scale)
    return o_f, o_c, o_p


def _layer(x, xc, c, c_ctx, ang_row, ang_col, w_mod, b_mod, norm_g, w_in, q_gain, k_gain,
           w_fourier, conv_w, conv_b, pool_w, pool_scale, w_out, ctx_out):
    b, s, _ = x.shape
    n_ctx = xc.shape[1]
    shift, scale, gate = jnp.split(jax.nn.silu(c) @ w_mod + b_mod, 3, axis=-1)
    shift_c, scale_c, gate_c = jnp.split(jax.nn.silu(c_ctx) @ w_mod + b_mod, 3, axis=-1)
    h = _rmsnorm(x, norm_g) * (1.0 + scale[:, None]) + shift[:, None]
    hc = _rmsnorm(xc, norm_g) * (1.0 + scale_c) + shift_c

    parts = _split_cols(h @ w_in)
    if ctx_out:
        parts_c = _split_cols(hc @ w_in)
        kc_raw, vc_raw = parts_c[1], parts_c[2]
    else:
        kc_raw, vc_raw = jnp.split(hc @ w_in[:, W_ATT:W_ATT + 2 * KV_DIM], 2, axis=-1)
    kc = _rmsnorm(kc_raw.reshape(b, n_ctx, N_KV_HEADS, HEAD_DIM), k_gain)
    vc = vc_raw.reshape(b, n_ctx, N_KV_HEADS, HEAD_DIM)

    q = _rmsnorm(parts[0].reshape(b, s, N_HEADS, HEAD_DIM), q_gain)
    k = _rmsnorm(parts[1].reshape(b, s, N_KV_HEADS, HEAD_DIM), k_gain)
    v = parts[2].reshape(b, s, N_KV_HEADS, HEAD_DIM)
    q = _axial_rope(q, ang_row, ang_col) * ATT_SCALE
    k = _axial_rope(k, ang_row, ang_col)
    o_att = jax.nn.silu(parts[3]) * _latent_attention(q, k, v, kc, vc)
    o_f, o_c, o_p = _local_branches(parts, w_fourier, conv_w, conv_b, pool_w, pool_scale)
    out = jnp.concatenate([o_f, o_att, o_c, o_p], axis=-1) @ w_out
    x_new = x + gate[:, None] * out

    if ctx_out:
        qc = _rmsnorm(parts_c[0].reshape(b, n_ctx, N_HEADS, HEAD_DIM), q_gain) * ATT_SCALE
        oc_att = jax.nn.silu(parts_c[3]) * _context_attention(qc, kc, vc)
        oc_f, oc_c, oc_p = _local_branches(parts_c, w_fourier, conv_w, conv_b, pool_w, pool_scale)
        out_c = jnp.concatenate([oc_f, oc_att, oc_c, oc_p], axis=-1) @ w_out
        xc = xc + gate_c * out_c
    return x_new, xc


def setup_inputs(seed: int = 0) -> dict:
    key = jax.random.key(seed)
    ks = jax.random.split(key, 20)
    f32 = jnp.float32
    nrm = lambda k, shape: jax.random.normal(k, shape, f32)
    return {
        "x": nrm(ks[0], (BATCH, SEQ, D_MODEL)),
        "c": nrm(ks[1], (BATCH, D_MODEL)),
        "ctx": nrm(ks[2], (BATCH, CTX_LEN, D_MODEL)),
        "c_ctx": nrm(ks[3], (D_MODEL,)),
        "w_mod": nrm(ks[4], (DEPTH, D_MODEL, 3 * D_MODEL)) * (0.5 * D_MODEL ** -0.5),
        "b_mod": nrm(ks[5], (DEPTH, 3 * D_MODEL)) * 0.02,
        "norm_g": 1.0 + 0.05 * nrm(ks[6], (DEPTH, D_MODEL)),
        "w_in": nrm(ks[7], (DEPTH, D_MODEL, D_IN)) * D_MODEL ** -0.5,
        "q_gain": 1.0 + 0.05 * nrm(ks[8], (DEPTH, HEAD_DIM)),
        "k_gain": 1.0 + 0.05 * nrm(ks[9], (DEPTH, HEAD_DIM)),
        "w_fourier": nrm(ks[10], (DEPTH, W_F, W_F)) * W_F ** -0.5,
        "conv_w": nrm(ks[11], (DEPTH, CONV_W, W_CONV)) * CONV_W ** -0.5,
        "conv_b": nrm(ks[12], (DEPTH, W_CONV)) * 0.02,
        "pool_w": nrm(ks[13], (DEPTH, len(POOL_WINDOWS), POOL_DIM, POOL_DIM)) * POOL_DIM ** -0.5,
        "pool_scale": 1.0 + 0.1 * nrm(ks[14], (DEPTH, W_POOL)),
        "w_out": nrm(ks[15], (DEPTH, D_MIX, D_MODEL)) * D_MIX ** -0.5,
    }


def reference(x, c, ctx, c_ctx, w_mod, b_mod, norm_g, w_in, q_gain, k_gain,
              w_fourier, conv_w, conv_b, pool_w, pool_scale, w_out):
    ang_row, ang_col = _axial_angles(x.shape[1])
    xc = ctx
    for l in range(DEPTH):
        x, xc = _layer(x, xc, c, c_ctx, ang_row, ang_col, w_mod[l], b_mod[l], norm_g[l], w_in[l],
                       q_gain[l], k_gain[l], w_fourier[l], conv_w[l], conv_b[l], pool_w[l],
                       pool_scale[l], w_out[l], ctx_out=(l < DEPTH - 1))
    return x
```

```python
import functools

import numpy as np
import jax
import jax.numpy as jnp
from jax import lax
from jax.experimental import pallas as pl
from jax.experimental.pallas import tpu as pltpu

D_MODEL = 1024
GRID_W = 64
W_GROUP = 256
HEAD_DIM = 64
N_HEADS = 4
N_KV_HEADS = 2
KV_DIM = N_KV_HEADS * HEAD_DIM
ATT_SCALE = HEAD_DIM ** -0.5
ROPE_FREQS = HEAD_DIM // 4
ROPE_THETA = 10000.0
F_DIM = 64
POOL_WINDOWS = (2, 4, 8, 16)
POOL_DIM = 64
POOL_HALO = 8
EPS = 1e-6
D_IN = 2816
C_Q, C_K, C_V, C_ZATT, C_UF, C_ZF, C_BC, C_CC, C_HC, C_ZC, C_UP, C_ZP = (
    0, 256, 384, 512, 768, 1024, 1280, 1536, 1792, 2048, 2304, 2560)

FFT_N1 = 128
FFT_N2 = 64

VMEM_LIMIT = 56 * 1024 * 1024
F32 = jnp.float32
BF16 = jnp.bfloat16
HI = lax.Precision.HIGHEST


def _silu(x):
    return x / (1.0 + jnp.exp(-x))


def _cparams(*sem):
    return pltpu.CompilerParams(dimension_semantics=sem, vmem_limit_bytes=VMEM_LIMIT)


def _rope_tables(n):
    pos = np.arange(n)
    row = (pos // GRID_W).astype(np.float64)
    col = (pos % GRID_W).astype(np.float64)
    inv = ROPE_THETA ** (-np.arange(ROPE_FREQS, dtype=np.float64) / ROPE_FREQS)
    ar, ac = row[:, None] * inv, col[:, None] * inv
    cos = np.concatenate([np.cos(ar), np.cos(ar), np.cos(ac), np.cos(ac)], axis=1)
    sin = np.concatenate([-np.sin(ar), np.sin(ar), -np.sin(ac), np.sin(ac)], axis=1)
    return (jnp.asarray(np.tile(cos, (1, 2)), F32), jnp.asarray(np.tile(sin, (1, 2)), F32))


def _dft_cs(n):
    ang = 2.0 * np.pi * np.outer(np.arange(n), np.arange(n)) / n
    return np.cos(ang), np.sin(ang)


def _fft_tables():
    c1, s1 = _dft_cs(FFT_N1)
    f1 = np.concatenate([c1, -s1], axis=0)
    c2, s2 = _dft_cs(FFT_N2)
    f2 = np.block([[c2, s2], [-s2, c2]])
    n = FFT_N1 * FFT_N2
    ang = 2.0 * np.pi * np.outer(np.arange(FFT_N2), np.arange(FFT_N1)) / n
    twc = np.repeat(np.cos(ang)[:, :, None], 128, axis=2)
    tws = np.repeat(np.sin(ang)[:, :, None], 128, axis=2)
    return (jnp.asarray(f1, F32), jnp.asarray(f2, F32), jnp.asarray(twc, F32), jnp.asarray(tws, F32))


def _channel_dft(n_pos):
    c, s = _dft_cs(F_DIM)
    scale = 1.0 / np.sqrt(float(n_pos) * F_DIM)
    eye = np.eye(W_GROUP // F_DIM)
    return (jnp.asarray(np.kron(eye, c) * scale, F32), jnp.asarray(np.kron(eye, s) * scale, F32))


def _group_mean_matrix():
    g = np.kron(np.eye(N_HEADS), np.full((HEAD_DIM, HEAD_DIM), 1.0 / HEAD_DIM))
    return jnp.asarray(g, BF16)


def _mod_kernel(c_ref, w_ref, b_ref, o_ref):
    a = _silu(c_ref[...]).astype(BF16)
    o_ref[...] = jnp.dot(a, w_ref[...], preferred_element_type=F32) + b_ref[...]


def _modulation(c, c_ctx, w_mod, b_mod):
    depth = w_mod.shape[0]
    b = c.shape[0]
    rows = jnp.concatenate([c, c_ctx[None], jnp.zeros((8 - b - 1, D_MODEL), F32)], axis=0)
    tn = 1024
    return pl.pallas_call(
        _mod_kernel,
        grid=(depth, 3 * D_MODEL // tn),
        in_specs=[pl.BlockSpec((8, D_MODEL), lambda l, j: (0, 0)),
                  pl.BlockSpec((None, D_MODEL, tn), lambda l, j: (l, 0, j)),
                  pl.BlockSpec((None, 1, tn), lambda l, j: (l, 0, j))],
        out_specs=pl.BlockSpec((None, 8, tn), lambda l, j: (l, 0, j)),
        out_shape=jax.ShapeDtypeStruct((depth, 8, 3 * D_MODEL), F32),
        compiler_params=_cparams("arbitrary", "arbitrary"),
        name="modulation",
    )(rows, w_mod.astype(BF16), b_mod[:, None, :])


def _head_norm(t, gmat, gain):
    sq = t * t
    hi = sq.astype(BF16)
    lo = (sq - hi.astype(F32)).astype(BF16)
    ms = (jnp.dot(hi, gmat, preferred_element_type=F32) + jnp.dot(lo, gmat, preferred_element_type=F32))
    return t * lax.rsqrt(ms + EPS) * gain


def _rope128(t, cos, sin):
    lane = lax.broadcasted_iota(jnp.int32, t.shape, 1)
    first = (lane % 32) < ROPE_FREQS
    partner = jnp.where(first, pltpu.roll(t, 128 - ROPE_FREQS, axis=1), pltpu.roll(t, ROPE_FREQS, axis=1))
    return t * cos + partner * sin


def _in_kernel(x_ref, shift_ref, scale_ref, g_ref, w_ref, qg_ref, kg_ref, gm_ref, cos_ref, sin_ref,
               q_ref, k_ref, v_ref, uf_ref, zz_ref, bc_ref, tu_ref, *, use_rope):
    x = x_ref[...]
    ms = jnp.mean(x * x, axis=-1, keepdims=True)
    y = x * lax.rsqrt(ms + EPS) * g_ref[...]
    hb = (y * (1.0 + scale_ref[...]) + shift_ref[...]).astype(BF16)

    def part(start, width=W_GROUP):
        return jnp.dot(hb, w_ref[:, start:start + width], preferred_element_type=F32)

    gm = gm_ref[...]
    q = _head_norm(part(C_Q), gm, qg_ref[...])
    k = _head_norm(part(C_K, KV_DIM), gm[:KV_DIM, :KV_DIM], kg_ref[...])
    if use_rope:
        cos, sin = cos_ref[...], sin_ref[...]
        q = jnp.concatenate([_rope128(q[:, :128], cos, sin), _rope128(q[:, 128:], cos, sin)], axis=1)
        k = _rope128(k, cos, sin)
    q_ref[...] = (q * ATT_SCALE).astype(BF16)
    k_ref[...] = k.astype(BF16)
    v_ref[...] = part(C_V, KV_DIM).astype(BF16)
    uf_ref[...] = part(C_UF)
    zz_ref[:, 0:256] = part(C_ZF)
    zz_ref[:, 256:512] = part(C_ZATT)
    zz_ref[:, 512:768] = part(C_ZC)
    zz_ref[:, 768:1024] = part(C_ZP)
    bc_ref[...] = part(C_BC)
    tu_ref[:, 0:256] = part(C_CC) * part(C_HC)
    tu_ref[:, 256:512] = part(C_UP)


def _in_proj(x, shift, scale, norm_g, w_in_b, q_gain, k_gain, gmat, cos_t, sin_t, *, use_rope, tm):
    b, s, _ = x.shape
    per_batch = shift.shape[0] > 1
    mod_map = (lambda bb, i: (bb, 0, 0)) if per_batch else (lambda bb, i: (0, 0, 0))
    const2 = lambda bb, i: (0, 0)
    tok = lambda w: pl.BlockSpec((None, tm, w), lambda bb, i: (bb, i, 0))
    out_dt = [(W_GROUP, BF16), (KV_DIM, BF16), (KV_DIM, BF16), (W_GROUP, F32), (4 * W_GROUP, F32),
              (W_GROUP, F32), (2 * W_GROUP, F32)]
    return pl.pallas_call(
        functools.partial(_in_kernel, use_rope=use_rope),
        grid=(b, s // tm),
        in_specs=[tok(D_MODEL),
                  pl.BlockSpec((None, 1, D_MODEL), mod_map),
                  pl.BlockSpec((None, 1, D_MODEL), mod_map),
                  pl.BlockSpec((1, D_MODEL), const2),
                  pl.BlockSpec((D_MODEL, D_IN), const2),
                  pl.BlockSpec((1, W_GROUP), const2),
                  pl.BlockSpec((1, KV_DIM), const2),
                  pl.BlockSpec((W_GROUP, W_GROUP), const2),
                  pl.BlockSpec((tm, 128), lambda bb, i: (i, 0)),
                  pl.BlockSpec((tm, 128), lambda bb, i: (i, 0))],
        out_specs=[tok(w) for w, _ in out_dt],
        out_shape=[jax.ShapeDtypeStruct((b, s, w), dt) for w, dt in out_dt],
        compiler_params=_cparams("arbitrary", "arbitrary"),
        name="in_proj_rope" if use_rope else "in_proj_ctx",
    )(x, shift, scale, norm_g[None, :], w_in_b, jnp.tile(q_gain, N_HEADS)[None, :],
      jnp.tile(k_gain, N_KV_HEADS)[None, :], gmat, cos_t, sin_t)


def _attn_kernel(q_ref, k_ref, v_ref, o_ref, m_scr, l_scr, acc_scr, *, tq, tk, n_chunks):
    lane = lax.broadcasted_iota(jnp.int32, (tq, 128), 1)
    for g in range(N_KV_HEADS):
        qb = q_ref[:, g * 128:(g + 1) * 128].astype(F32)
        swapped = pltpu.roll(qb, HEAD_DIM, axis=1)
        keep = (lane // HEAD_DIM) == g
        head_a, head_b = (qb, swapped) if g == 0 else (swapped, qb)
        qs = jnp.concatenate([jnp.where(keep, head_a, 0.0), jnp.where(keep, head_b, 0.0)],
                             axis=0).astype(BF16)
        m_scr[...] = jnp.full(m_scr.shape, -jnp.inf, F32)
        l_scr[...] = jnp.zeros(l_scr.shape, F32)
        acc_scr[...] = jnp.zeros(acc_scr.shape, F32)

        def chunk(c, carry):
            start = pl.multiple_of(c * tk, tk)
            kc = k_ref[pl.ds(start, tk), :]
            vc = v_ref[pl.ds(start, tk), :]
            s = lax.dot_general(qs, kc, (((1,), (1,)), ((), ())), preferred_element_type=F32)
            m_prev = m_scr[...]
            m_new = jnp.maximum(m_prev, jnp.max(s, axis=1, keepdims=True))
            alpha = jnp.exp(m_prev - m_new)
            p = jnp.exp(s - m_new)
            l_scr[...] = alpha * l_scr[...] + jnp.sum(p, axis=1, keepdims=True)
            acc_scr[...] = alpha * acc_scr[...] + jnp.dot(p.astype(BF16), vc, preferred_element_type=F32)
            m_scr[...] = m_new
            return carry

        lax.fori_loop(0, n_chunks, chunk, 0)
        o = acc_scr[...] / l_scr[...]
        oa, ob = o[:tq], o[tq:]
        oa_sw, ob_sw = pltpu.roll(oa, HEAD_DIM, axis=1), pltpu.roll(ob, HEAD_DIM, axis=1)
        left, right = (oa, ob_sw) if g == 0 else (oa_sw, ob)
        o_ref[:, g * 128:(g + 1) * 128] = jnp.where(lane < HEAD_DIM, left, right)


def _attention(q, k, v, *, tq, tk):
    b, sq, _ = q.shape
    skv = k.shape[1]
    kern = functools.partial(_attn_kernel, tq=tq, tk=tk, n_chunks=skv // tk)
    return pl.pallas_call(
        kern,
        grid=(b, sq // tq),
        in_specs=[pl.BlockSpec((None, tq, W_GROUP), lambda bb, i: (bb, i, 0)),
                  pl.BlockSpec((None, skv, KV_DIM), lambda bb, i: (bb, 0, 0)),
                  pl.BlockSpec((None, skv, KV_DIM), lambda bb, i: (bb, 0, 0))],
        out_specs=pl.BlockSpec((None, tq, W_GROUP), lambda bb, i: (bb, i, 0)),
        out_shape=jax.ShapeDtypeStruct((b, sq, W_GROUP), F32),
        scratch_shapes=[pltpu.VMEM((2 * tq, 1), F32), pltpu.VMEM((2 * tq, 1), F32),
                        pltpu.VMEM((2 * tq, 128), F32)],
        compiler_params=_cparams("arbitrary", "arbitrary"),
        name="attention",
    )(q, k, v)


def _fft1_kernel(x_ref, f1_ref, twc_ref, tws_ref, tr_ref, ti_ref, *, nb):
    t = jnp.dot(f1_ref[...], x_ref[...], precision=HI, preferred_element_type=F32)
    for j in range(nb):
        tr = t[:FFT_N1, j * W_GROUP:(j + 1) * W_GROUP]
        ti = t[FFT_N1:, j * W_GROUP:(j + 1) * W_GROUP]
        c = jnp.concatenate([twc_ref[j], twc_ref[j]], axis=1)
        s = jnp.concatenate([tws_ref[j], tws_ref[j]], axis=1)
        tr_ref[j] = tr * c + ti * s
        ti_ref[j] = ti * c - tr * s


def _fft2_kernel(tr_ref, ti_ref, f2_ref, pr_ref, pi_ref):
    xin = jnp.concatenate([tr_ref[...], ti_ref[...]], axis=0)
    o = jnp.dot(f2_ref[...], xin, precision=HI, preferred_element_type=F32)
    pr_ref[...] = o[:FFT_N2]
    pi_ref[...] = o[FFT_N2:]


def _position_dft(u, tables):
    f1, f2, twc, tws = tables
    b, n, w = u.shape
    assert n == FFT_N1 * FFT_N2 and w == W_GROUP
    nb = 8
    x2d = u.reshape(b, FFT_N1, FFT_N2 * w)
    t_shape = jax.ShapeDtypeStruct((b, FFT_N2, FFT_N1, w), F32)
    tr, ti = pl.pallas_call(
        functools.partial(_fft1_kernel, nb=nb),
        grid=(FFT_N2 // nb, b),
        in_specs=[pl.BlockSpec((None, FFT_N1, nb * w), lambda j, bb: (bb, 0, j)),
                  pl.BlockSpec((2 * FFT_N1, FFT_N1), lambda j, bb: (0, 0)),
                  pl.BlockSpec((nb, FFT_N1, 128), lambda j, bb: (j, 0, 0)),
                  pl.BlockSpec((nb, FFT_N1, 128), lambda j, bb: (j, 0, 0))],
        out_specs=[pl.BlockSpec((None, nb, FFT_N1, w), lambda j, bb: (bb, j, 0, 0))] * 2,
        out_shape=[t_shape, t_shape],
        compiler_params=_cparams("arbitrary", "arbitrary"),
        name="fft_stage1",
    )(x2d, f1, twc, tws)
    ln = 4096
    width = FFT_N1 * w
    o_shape = jax.ShapeDtypeStruct((b, FFT_N2, width), F32)
    pr, pi = pl.pallas_call(
        _fft2_kernel,
        grid=(b, width // ln),
        in_specs=[pl.BlockSpec((None, FFT_N2, ln), lambda bb, j: (bb, 0, j)),
                  pl.BlockSpec((None, FFT_N2, ln), lambda bb, j: (bb, 0, j)),
                  pl.BlockSpec((2 * FFT_N2, 2 * FFT_N2), lambda bb, j: (0, 0))],
        out_specs=[pl.BlockSpec((None, FFT_N2, ln), lambda bb, j: (bb, 0, j))] * 2,
        out_shape=[o_shape, o_shape],
        compiler_params=_cparams("arbitrary", "arbitrary"),
        name="fft_stage2",
    )(tr.reshape(b, FFT_N2, width), ti.reshape(b, FFT_N2, width), f2)
    return pr.reshape(b, n, w), pi.reshape(b, n, w)


def _ctx_dft_kernel(u_ref, f_ref, pr_ref, pi_ref):
    n = u_ref.shape[0]
    o = jnp.dot(f_ref[...], u_ref[...], precision=HI, preferred_element_type=F32)
    pr_ref[...] = o[:n]
    pi_ref[...] = o[n:]


def _ctx_position_dft(u):
    b, n, w = u.shape
    c, s = _dft_cs(n)
    f = jnp.asarray(np.concatenate([c, -s], axis=0), F32)
    shape = jax.ShapeDtypeStruct((b, n, w), F32)
    spec = pl.BlockSpec((None, n, w), lambda bb: (bb, 0, 0))
    return pl.pallas_call(
        _ctx_dft_kernel,
        grid=(b,),
        in_specs=[spec, pl.BlockSpec((2 * n, n), lambda bb: (0, 0))],
        out_specs=[spec, spec],
        out_shape=[shape, shape],
        compiler_params=_cparams("arbitrary"),
        name="ctx_dft",
    )(u, f)


def _out_kernel(x_ref, gate_ref, zz_ref, bc_ref, tu_ref, tu_prev_ref, tu_next_ref, att_ref, pr_ref, pi_ref,
                cc_ref, sc_ref, wf_ref, cw_ref, cb_ref, wp_ref, ps_ref, wo_ref, o_ref, ext_scr, *, tm, n_tok):
    i = pl.program_id(1)
    last = pl.num_programs(1) - 1
    h = POOL_HALO
    ext_scr[0:h, :] = jnp.where(i > 0, tu_prev_ref[...], 0.0)
    ext_scr[h:h + tm, :] = tu_ref[...]
    ext_scr[h + tm:, :] = jnp.where(i < last, tu_next_ref[...], 0.0)

    def shifted(d, lo, width):
        return ext_scr[pl.ds(h + d, tm), lo:lo + width]

    gates = _silu(zz_ref[...])

    y = (jnp.dot(pr_ref[...], cc_ref[...], precision=HI, preferred_element_type=F32)
         + jnp.dot(pi_ref[...], sc_ref[...], precision=HI, preferred_element_type=F32))
    o_f = jnp.dot(y.astype(BF16), wf_ref[...], preferred_element_type=F32)

    cw = cw_ref[...]
    yc = (shifted(-1, 0, W_GROUP) * cw[0:1] + shifted(0, 0, W_GROUP) * cw[1:2]
          + shifted(1, 0, W_GROUP) * cw[2:3] + cb_ref[...])
    o_c = bc_ref[...] * yc

    def wsum(lo, d_from, d_to):
        acc = shifted(d_from, lo, 128)
        for d in range(d_from + 1, d_to + 1):
            acc = acc + shifted(d, lo, 128)
        return acc

    s2 = wsum(W_GROUP, -1, 0)
    s4 = s2 + shifted(-2, W_GROUP, 128) + shifted(1, W_GROUP, 128)
    s8 = wsum(W_GROUP + 128, -4, 3)
    s16 = s8 + wsum(W_GROUP + 128, -8, -5) + wsum(W_GROUP + 128, 4, 7)
    lane128 = lax.broadcasted_iota(jnp.int32, (tm, 128), 1)
    sums = jnp.concatenate([jnp.where(lane128 < POOL_DIM, s2, s4), jnp.where(lane128 < POOL_DIM, s8, s16)],
                           axis=1)
    lane = lax.broadcasted_iota(jnp.int32, (tm, W_GROUP), 1)
    pos = i * tm + lax.broadcasted_iota(jnp.int32, (tm, W_GROUP), 0)
    half = jnp.left_shift(1, lane // POOL_DIM)
    cnt = jnp.minimum(pos + half, n_tok) - jnp.maximum(pos - half, 0)
    d = sums / cnt.astype(F32) - shifted(0, W_GROUP, W_GROUP)
    o_p = jnp.dot(d.astype(BF16), wp_ref[...], preferred_element_type=F32) * ps_ref[...]

    mixed = jnp.concatenate([o_f, att_ref[...], o_c, o_p], axis=1) * gates
    out = jnp.dot(mixed.astype(BF16), wo_ref[...], preferred_element_type=F32)
    o_ref[...] = x_ref[...] + gate_ref[...] * out


def _out_proj(x, gate, zz, bc, tu, att, pr, pi, cc, sc, wf_b, conv_w, conv_b, wp_b, pool_scale, wo_b, *, tm):
    b, s, _ = x.shape
    per_batch = gate.shape[0] > 1
    mod_map = (lambda bb, i: (bb, 0, 0)) if per_batch else (lambda bb, i: (0, 0, 0))
    const2 = lambda bb, i: (0, 0)
    tok = lambda w: pl.BlockSpec((None, tm, w), lambda bb, i: (bb, i, 0))
    hb = tm // POOL_HALO
    n_hblk = s // POOL_HALO
    full = lambda a: pl.BlockSpec(a.shape, const2)
    cw8 = jnp.concatenate([conv_w, jnp.zeros((8 - conv_w.shape[0], W_GROUP), F32)], axis=0)
    return pl.pallas_call(
        functools.partial(_out_kernel, tm=tm, n_tok=s),
        grid=(b, s // tm),
        in_specs=[tok(D_MODEL),
                  pl.BlockSpec((None, 1, D_MODEL), mod_map),
                  tok(4 * W_GROUP), tok(W_GROUP), tok(2 * W_GROUP),
                  pl.BlockSpec((None, POOL_HALO, 2 * W_GROUP),
                               lambda bb, i: (bb, jnp.maximum(i * hb - 1, 0), 0)),
                  pl.BlockSpec((None, POOL_HALO, 2 * W_GROUP),
                               lambda bb, i: (bb, jnp.minimum((i + 1) * hb, n_hblk - 1), 0)),
                  tok(W_GROUP), tok(W_GROUP), tok(W_GROUP),
                  full(cc), full(sc), full(wf_b), full(cw8),
                  pl.BlockSpec((1, W_GROUP), const2), full(wp_b),
                  pl.BlockSpec((1, W_GROUP), const2), full(wo_b)],
        out_specs=tok(D_MODEL),
        out_shape=jax.ShapeDtypeStruct((b, s, D_MODEL), F32),
        scratch_shapes=[pltpu.VMEM((tm + 2 * POOL_HALO, 2 * W_GROUP), F32)],
        compiler_params=_cparams("arbitrary", "arbitrary"),
        name="out_proj",
    )(x, gate, zz, bc, tu, tu, tu, att, pr, pi, cc, sc, wf_b, cw8, conv_b[None, :], wp_b,
      pool_scale[None, :], wo_b)


def _block_diag(blocks):
    n = blocks.shape[0]
    rows = [jnp.concatenate([blocks[i] if j == i else jnp.zeros_like(blocks[i]) for j in range(n)], axis=1)
            for i in range(n)]
    return jnp.concatenate(rows, axis=0)


def kernel(x, c, ctx, c_ctx, w_mod, b_mod, norm_g, w_in, q_gain, k_gain, w_fourier, conv_w, conv_b, pool_w,
           pool_scale, w_out):
    depth = w_mod.shape[0]
    b, s, _ = x.shape
    n_ctx = ctx.shape[1]
    mod = _modulation(c, c_ctx, w_mod, b_mod)
    cos_t, sin_t = _rope_tables(s)
    gmat = _group_mean_matrix()
    fft_tables = _fft_tables()
    cc_lat, sc_lat = _channel_dft(s)
    cc_ctx, sc_ctx = _channel_dft(n_ctx)

    xc = ctx
    for l in range(depth):
        ctx_out = l < depth - 1
        shift, scale, gate = (mod[l, :b, j * D_MODEL:(j + 1) * D_MODEL][:, None, :] for j in range(3))
        shift_c, scale_c, gate_c = (mod[l, b:b + 1, j * D_MODEL:(j + 1) * D_MODEL][:, None, :] for j in range(3))
        w_in_b = w_in[l].astype(BF16)
        wo_b = w_out[l].astype(BF16)
        wf_b = w_fourier[l].astype(BF16)
        wp_b = _block_diag(pool_w[l]).astype(BF16)

        q, k, v, uf, zz, bc, tu = _in_proj(x, shift, scale, norm_g[l], w_in_b, q_gain[l], k_gain[l], gmat,
                                           cos_t, sin_t, use_rope=True, tm=512)
        qc, kc, vc, ufc, zzc, bcc, tuc = _in_proj(xc, shift_c, scale_c, norm_g[l], w_in_b, q_gain[l], k_gain[l],
                                                  gmat, cos_t[:n_ctx], sin_t[:n_ctx], use_rope=False, tm=n_ctx)
        k_all = jnp.concatenate([kc, k], axis=1)
        v_all = jnp.concatenate([vc, v], axis=1)
        att = _attention(q, k_all, v_all, tq=256, tk=768)
        pr, pi = _position_dft(uf, fft_tables)
        x_new = _out_proj(x, gate, zz, bc, tu, att, pr, pi, cc_lat, sc_lat, wf_b, conv_w[l], conv_b[l], wp_b,
                          pool_scale[l], wo_b, tm=256)
        if ctx_out:
            att_c = _attention(qc, kc, vc, tq=n_ctx, tk=n_ctx)
            prc, pic = _ctx_position_dft(ufc)
            xc = _out_proj(xc, gate_c, zzc, bcc, tuc, att_c, prc, pic, cc_ctx, sc_ctx, wf_b, conv_w[l], conv_b[l],
                           wp_b, pool_scale[l], wo_b, tm=n_ctx)
        x = x_new
    return x
```

```python
import functools

import numpy as np
import jax
import jax.numpy as jnp
from jax import lax
from jax.experimental import pallas as pl
from jax.experimental.pallas import tpu as pltpu

D_MODEL = 1024
GRID_W = 64
W_GROUP = 256
HEAD_DIM = 64
N_HEADS = 4
N_KV_HEADS = 2
KV_DIM = N_KV_HEADS * HEAD_DIM
ATT_SCALE = HEAD_DIM ** -0.5
Q_SCALE = ATT_SCALE * float(np.log2(np.e))
ROPE_FREQS = HEAD_DIM // 4
ROPE_THETA = 10000.0
F_DIM = 64
POOL_WINDOWS = (2, 4, 8, 16)
POOL_DIM = 64
POOL_HALO = 8
EPS = 1e-6
D_IN = 2816
C_Q, C_K, C_V, C_ZATT, C_UF, C_ZF, C_BC, C_CC, C_HC, C_ZC, C_UP, C_ZP = (
    0, 256, 384, 512, 768, 1024, 1280, 1536, 1792, 2048, 2304, 2560)

FFT_N1 = 128
FFT_N2 = 64

VMEM_LIMIT = 56 * 1024 * 1024
F32 = jnp.float32
BF16 = jnp.bfloat16
HI = lax.Precision.HIGHEST


def _silu(x):
    return x / (1.0 + jnp.exp(-x))


def _cparams(*sem):
    return pltpu.CompilerParams(dimension_semantics=sem, vmem_limit_bytes=VMEM_LIMIT)


def _rope_tables(n):
    pos = np.arange(n)
    row = (pos // GRID_W).astype(np.float64)
    col = (pos % GRID_W).astype(np.float64)
    inv = ROPE_THETA ** (-np.arange(ROPE_FREQS, dtype=np.float64) / ROPE_FREQS)
    ar, ac = row[:, None] * inv, col[:, None] * inv
    cos = np.concatenate([np.cos(ar), np.cos(ar), np.cos(ac), np.cos(ac)], axis=1)
    sin = np.concatenate([-np.sin(ar), np.sin(ar), -np.sin(ac), np.sin(ac)], axis=1)
    return (jnp.asarray(np.tile(cos, (1, 2)), F32), jnp.asarray(np.tile(sin, (1, 2)), F32))


def _dft_cs(n):
    ang = 2.0 * np.pi * np.outer(np.arange(n), np.arange(n)) / n
    return np.cos(ang), np.sin(ang)


def _fft_tables():
    c1, s1 = _dft_cs(FFT_N1)
    f1 = np.concatenate([c1, -s1], axis=0)
    c2, s2 = _dft_cs(FFT_N2)
    f2 = np.block([[c2, s2], [-s2, c2]])
    n = FFT_N1 * FFT_N2
    ang = 2.0 * np.pi * np.outer(np.arange(FFT_N2), np.arange(FFT_N1)) / n
    twc = np.repeat(np.cos(ang)[:, :, None], 128, axis=2)
    tws = np.repeat(np.sin(ang)[:, :, None], 128, axis=2)
    return (jnp.asarray(f1, F32), jnp.asarray(f2, F32), jnp.asarray(twc, F32), jnp.asarray(tws, F32))


def _channel_dft(n_pos):
    c, s = _dft_cs(F_DIM)
    scale = 1.0 / np.sqrt(float(n_pos) * F_DIM)
    eye = np.eye(W_GROUP // F_DIM)
    return (jnp.asarray(np.kron(eye, c) * scale, F32), jnp.asarray(np.kron(eye, s) * scale, F32))


def _group_mean_matrix():
    g = np.kron(np.eye(N_HEADS), np.full((HEAD_DIM, HEAD_DIM), 1.0 / HEAD_DIM))
    return jnp.asarray(g, BF16)


def _mod_kernel(c_ref, w_ref, b_ref, o_ref):
    a = _silu(c_ref[...]).astype(BF16)
    o_ref[...] = jnp.dot(a, w_ref[...], preferred_element_type=F32) + b_ref[...]


def _modulation(c, c_ctx, w_mod, b_mod):
    depth = w_mod.shape[0]
    b = c.shape[0]
    rows = jnp.concatenate([c, c_ctx[None], jnp.zeros((8 - b - 1, D_MODEL), F32)], axis=0)
    tn = 1024
    return pl.pallas_call(
        _mod_kernel,
        grid=(depth, 3 * D_MODEL // tn),
        in_specs=[pl.BlockSpec((8, D_MODEL), lambda l, j: (0, 0)),
                  pl.BlockSpec((None, D_MODEL, tn), lambda l, j: (l, 0, j)),
                  pl.BlockSpec((None, 1, tn), lambda l, j: (l, 0, j))],
        out_specs=pl.BlockSpec((None, 8, tn), lambda l, j: (l, 0, j)),
        out_shape=jax.ShapeDtypeStruct((depth, 8, 3 * D_MODEL), F32),
        compiler_params=_cparams("arbitrary", "arbitrary"),
        name="modulation",
    )(rows, w_mod.astype(BF16), b_mod[:, None, :])


def _head_norm(t, gmat, gain):
    sq = t * t
    hi = sq.astype(BF16)
    lo = (sq - hi.astype(F32)).astype(BF16)
    ms = (jnp.dot(hi, gmat, preferred_element_type=F32) + jnp.dot(lo, gmat, preferred_element_type=F32))
    return t * lax.rsqrt(ms + EPS) * gain


def _rope128(t, cos, sin):
    lane = lax.broadcasted_iota(jnp.int32, t.shape, 1)
    first = (lane % 32) < ROPE_FREQS
    partner = jnp.where(first, pltpu.roll(t, 128 - ROPE_FREQS, axis=1), pltpu.roll(t, ROPE_FREQS, axis=1))
    return t * cos + partner * sin


def _in_kernel(x_ref, shift_ref, scale_ref, g_ref, w_ref, qg_ref, kg_ref, gm_ref, cos_ref, sin_ref,
               q_ref, k_ref, v_ref, uf_ref, zz_ref, bc_ref, tu_ref, *, use_rope):
    x = x_ref[...]
    ms = jnp.mean(x * x, axis=-1, keepdims=True)
    y = x * lax.rsqrt(ms + EPS) * g_ref[...]
    hb = (y * (1.0 + scale_ref[...]) + shift_ref[...]).astype(BF16)

    def part(start, width=W_GROUP):
        return jnp.dot(hb, w_ref[:, start:start + width], preferred_element_type=F32)

    gm = gm_ref[...]
    q = _head_norm(part(C_Q), gm, qg_ref[...])
    k = _head_norm(part(C_K, KV_DIM), gm[:KV_DIM, :KV_DIM], kg_ref[...])
    if use_rope:
        cos, sin = cos_ref[...], sin_ref[...]
        q = jnp.concatenate([_rope128(q[:, :128], cos, sin), _rope128(q[:, 128:], cos, sin)], axis=1)
        k = _rope128(k, cos, sin)
    q_ref[...] = (q * Q_SCALE).astype(BF16)
    k_ref[...] = k.astype(BF16)
    v_ref[...] = part(C_V, KV_DIM).astype(BF16)
    uf_ref[...] = part(C_UF)
    zz_ref[:, 0:256] = part(C_ZF)
    zz_ref[:, 256:512] = part(C_ZATT)
    zz_ref[:, 512:768] = part(C_ZC)
    zz_ref[:, 768:1024] = part(C_ZP)
    bc_ref[...] = part(C_BC)
    tu_ref[:, 0:256] = part(C_CC) * part(C_HC)
    tu_ref[:, 256:512] = part(C_UP)


def _in_proj(x, shift, scale, norm_g, w_in_b, q_gain, k_gain, gmat, cos_t, sin_t, *, use_rope, tm):
    b, s, _ = x.shape
    per_batch = shift.shape[0] > 1
    mod_map = (lambda bb, i: (bb, 0, 0)) if per_batch else (lambda bb, i: (0, 0, 0))
    const2 = lambda bb, i: (0, 0)
    tok = lambda w: pl.BlockSpec((None, tm, w), lambda bb, i: (bb, i, 0))
    out_dt = [(W_GROUP, BF16), (KV_DIM, BF16), (KV_DIM, BF16), (W_GROUP, F32), (4 * W_GROUP, F32),
              (W_GROUP, F32), (2 * W_GROUP, F32)]
    return pl.pallas_call(
        functools.partial(_in_kernel, use_rope=use_rope),
        grid=(b, s // tm),
        in_specs=[tok(D_MODEL),
                  pl.BlockSpec((None, 1, D_MODEL), mod_map),
                  pl.BlockSpec((None, 1, D_MODEL), mod_map),
                  pl.BlockSpec((1, D_MODEL), const2),
                  pl.BlockSpec((D_MODEL, D_IN), const2),
                  pl.BlockSpec((1, W_GROUP), const2),
                  pl.BlockSpec((1, KV_DIM), const2),
                  pl.BlockSpec((W_GROUP, W_GROUP), const2),
                  pl.BlockSpec((tm, 128), lambda bb, i: (i, 0)),
                  pl.BlockSpec((tm, 128), lambda bb, i: (i, 0))],
        out_specs=[tok(w) for w, _ in out_dt],
        out_shape=[jax.ShapeDtypeStruct((b, s, w), dt) for w, dt in out_dt],
        compiler_params=_cparams("arbitrary", "arbitrary"),
        name="in_proj_rope" if use_rope else "in_proj_ctx",
    )(x, shift, scale, norm_g[None, :], w_in_b, jnp.tile(q_gain, N_HEADS)[None, :],
      jnp.tile(k_gain, N_KV_HEADS)[None, :], gmat, cos_t, sin_t)


def _attn_kernel(q_ref, k_ref, v_ref, o_ref, qs_scr, s_scr, mc_scr, m_scr, l_scr, acc_scr, *, tq, tk, n_chunks):
    assert n_chunks % 2 == 1
    lane = lax.broadcasted_iota(jnp.int32, (tq, 128), 1)

    def scores(c, slot):
        start = c * tk if isinstance(c, int) else pl.multiple_of(c * tk, tk)
        s = lax.dot_general(qs_scr[...], k_ref[pl.ds(start, tk), :], (((1,), (1,)), ((), ())),
                            preferred_element_type=F32)
        s_scr[slot] = s
        mc_scr[slot] = jnp.max(s, axis=1, keepdims=True)

    def softmax_pv(c, slot):
        start = c * tk if isinstance(c, int) else pl.multiple_of(c * tk, tk)
        m_prev = m_scr[...]
        m_new = jnp.maximum(m_prev, mc_scr[slot])
        alpha = jnp.exp2(m_prev - m_new)
        p = jnp.exp2(s_scr[slot] - m_new)
        l_scr[...] = alpha * l_scr[...] + jnp.sum(p, axis=1, keepdims=True)
        acc_scr[...] = alpha * acc_scr[...] + jnp.dot(p.astype(BF16), v_ref[pl.ds(start, tk), :],
                                                      preferred_element_type=F32)
        m_scr[...] = m_new

    for g in range(N_KV_HEADS):
        qb = q_ref[:, g * 128:(g + 1) * 128].astype(F32)
        swapped = pltpu.roll(qb, HEAD_DIM, axis=1)
        keep = (lane // HEAD_DIM) == g
        head_a, head_b = (qb, swapped) if g == 0 else (swapped, qb)
        qs_scr[...] = jnp.concatenate([jnp.where(keep, head_a, 0.0), jnp.where(keep, head_b, 0.0)],
                                      axis=0).astype(BF16)
        m_scr[...] = jnp.full(m_scr.shape, -jnp.inf, F32)
        l_scr[...] = jnp.zeros(l_scr.shape, F32)
        acc_scr[...] = jnp.zeros(acc_scr.shape, F32)

        scores(0, 0)

        def pair(i, carry):
            scores(2 * i + 1, 1)
            softmax_pv(2 * i, 0)
            scores(2 * i + 2, 0)
            softmax_pv(2 * i + 1, 1)
            return carry

        lax.fori_loop(0, n_chunks // 2, pair, 0)
        softmax_pv(n_chunks - 1, 0)
        o = acc_scr[...] / l_scr[...]
        oa, ob = o[:tq], o[tq:]
        oa_sw, ob_sw = pltpu.roll(oa, HEAD_DIM, axis=1), pltpu.roll(ob, HEAD_DIM, axis=1)
        left, right = (oa, ob_sw) if g == 0 else (oa_sw, ob)
        o_ref[:, g * 128:(g + 1) * 128] = jnp.where(lane < HEAD_DIM, left, right)


def _attention(q, k, v, *, tq, tk):
    b, sq, _ = q.shape
    skv = k.shape[1]
    kern = functools.partial(_attn_kernel, tq=tq, tk=tk, n_chunks=skv // tk)
    return pl.pallas_call(
        kern,
        grid=(b, sq // tq),
        in_specs=[pl.BlockSpec((None, tq, W_GROUP), lambda bb, i: (bb, i, 0)),
                  pl.BlockSpec((None, skv, KV_DIM), lambda bb, i: (bb, 0, 0)),
                  pl.BlockSpec((None, skv, KV_DIM), lambda bb, i: (bb, 0, 0))],
        out_specs=pl.BlockSpec((None, tq, W_GROUP), lambda bb, i: (bb, i, 0)),
        out_shape=jax.ShapeDtypeStruct((b, sq, W_GROUP), F32),
        scratch_shapes=[pltpu.VMEM((2 * tq, 128), BF16), pltpu.VMEM((2, 2 * tq, tk), F32),
                        pltpu.VMEM((2, 2 * tq, 1), F32), pltpu.VMEM((2 * tq, 1), F32),
                        pltpu.VMEM((2 * tq, 1), F32), pltpu.VMEM((2 * tq, 128), F32)],
        compiler_params=_cparams("arbitrary", "arbitrary"),
        name="attention",
    )(q, k, v)


def _fft1_kernel(x_ref, f1_ref, twc_ref, tws_ref, tr_ref, ti_ref, *, nb):
    t = jnp.dot(f1_ref[...], x_ref[...], precision=HI, preferred_element_type=F32)
    for j in range(nb):
        tr = t[:FFT_N1, j * W_GROUP:(j + 1) * W_GROUP]
        ti = t[FFT_N1:, j * W_GROUP:(j + 1) * W_GROUP]
        c = jnp.concatenate([twc_ref[j], twc_ref[j]], axis=1)
        s = jnp.concatenate([tws_ref[j], tws_ref[j]], axis=1)
        tr_ref[j] = tr * c + ti * s
        ti_ref[j] = ti * c - tr * s


def _fft2_kernel(tr_ref, ti_ref, f2_ref, pr_ref, pi_ref):
    xin = jnp.concatenate([tr_ref[...], ti_ref[...]], axis=0)
    o = jnp.dot(f2_ref[...], xin, precision=HI, preferred_element_type=F32)
    pr_ref[...] = o[:FFT_N2]
    pi_ref[...] = o[FFT_N2:]


def _position_dft(u, tables):
    f1, f2, twc, tws = tables
    b, n, w = u.shape
    assert n == FFT_N1 * FFT_N2 and w == W_GROUP
    nb = 8
    x2d = u.reshape(b, FFT_N1, FFT_N2 * w)
    t_shape = jax.ShapeDtypeStruct((b, FFT_N2, FFT_N1, w), F32)
    tr, ti = pl.pallas_call(
        functools.partial(_fft1_kernel, nb=nb),
        grid=(FFT_N2 // nb, b),
        in_specs=[pl.BlockSpec((None, FFT_N1, nb * w), lambda j, bb: (bb, 0, j)),
                  pl.BlockSpec((2 * FFT_N1, FFT_N1), lambda j, bb: (0, 0)),
                  pl.BlockSpec((nb, FFT_N1, 128), lambda j, bb: (j, 0, 0)),
                  pl.BlockSpec((nb, FFT_N1, 128), lambda j, bb: (j, 0, 0))],
        out_specs=[pl.BlockSpec((None, nb, FFT_N1, w), lambda j, bb: (bb, j, 0, 0))] * 2,
        out_shape=[t_shape, t_shape],
        compiler_params=_cparams("arbitrary", "arbitrary"),
        name="fft_stage1",
    )(x2d, f1, twc, tws)
    ln = 4096
    width = FFT_N1 * w
    o_shape = jax.ShapeDtypeStruct((b, FFT_N2, width), F32)
    pr, pi = pl.pallas_call(
        _fft2_kernel,
        grid=(b, width // ln),
        in_specs=[pl.BlockSpec((None, FFT_N2, ln), lambda bb, j: (bb, 0, j)),
                  pl.BlockSpec((None, FFT_N2, ln), lambda bb, j: (bb, 0, j)),
                  pl.BlockSpec((2 * FFT_N2, 2 * FFT_N2), lambda bb, j: (0, 0))],
        out_specs=[pl.BlockSpec((None, FFT_N2, ln), lambda bb, j: (bb, 0, j))] * 2,
        out_shape=[o_shape, o_shape],
        compiler_params=_cparams("arbitrary", "arbitrary"),
        name="fft_stage2",
    )(tr.reshape(b, FFT_N2, width), ti.reshape(b, FFT_N2, width), f2)
    return pr.reshape(b, n, w), pi.reshape(b, n, w)


def _ctx_dft_kernel(u_ref, f_ref, pr_ref, pi_ref):
    n = u_ref.shape[0]
    o = jnp.dot(f_ref[...], u_ref[...], precision=HI, preferred_element_type=F32)
    pr_ref[...] = o[:n]
    pi_ref[...] = o[n:]


def _ctx_position_dft(u):
    b, n, w = u.shape
    c, s = _dft_cs(n)
    f = jnp.asarray(np.concatenate([c, -s], axis=0), F32)
    shape = jax.ShapeDtypeStruct((b, n, w), F32)
    spec = pl.BlockSpec((None, n, w), lambda bb: (bb, 0, 0))
    return pl.pallas_call(
        _ctx_dft_kernel,
        grid=(b,),
        in_specs=[spec, pl.BlockSpec((2 * n, n), lambda bb: (0, 0))],
        out_specs=[spec, spec],
        out_shape=[shape, shape],
        compiler_params=_cparams("arbitrary"),
        name="ctx_dft",
    )(u, f)


def _out_kernel(x_ref, gate_ref, zz_ref, bc_ref, tu_ref, tu_prev_ref, tu_next_ref, att_ref, pr_ref, pi_ref,
                cc_ref, sc_ref, wf_ref, cw_ref, cb_ref, wp_ref, ps_ref, wo_ref, o_ref, ext_scr, *, tm, n_tok):
    i = pl.program_id(1)
    last = pl.num_programs(1) - 1
    h = POOL_HALO
    ext_scr[0:h, :] = jnp.where(i > 0, tu_prev_ref[...], 0.0)
    ext_scr[h:h + tm, :] = tu_ref[...]
    ext_scr[h + tm:, :] = jnp.where(i < last, tu_next_ref[...], 0.0)

    def shifted(d, lo, width):
        return ext_scr[pl.ds(h + d, tm), lo:lo + width]

    gates = _silu(zz_ref[...])

    y = (jnp.dot(pr_ref[...], cc_ref[...], precision=HI, preferred_element_type=F32)
         + jnp.dot(pi_ref[...], sc_ref[...], precision=HI, preferred_element_type=F32))
    o_f = jnp.dot(y.astype(BF16), wf_ref[...], preferred_element_type=F32)

    cw = cw_ref[...]
    yc = (shifted(-1, 0, W_GROUP) * cw[0:1] + shifted(0, 0, W_GROUP) * cw[1:2]
          + shifted(1, 0, W_GROUP) * cw[2:3] + cb_ref[...])
    o_c = bc_ref[...] * yc

    def wsum(lo, d_from, d_to):
        acc = shifted(d_from, lo, 128)
        for d in range(d_from + 1, d_to + 1):
            acc = acc + shifted(d, lo, 128)
        return acc

    s2 = wsum(W_GROUP, -1, 0)
    s4 = s2 + shifted(-2, W_GROUP, 128) + shifted(1, W_GROUP, 128)
    s8 = wsum(W_GROUP + 128, -4, 3)
    s16 = s8 + wsum(W_GROUP + 128, -8, -5) + wsum(W_GROUP + 128, 4, 7)
    lane128 = lax.broadcasted_iota(jnp.int32, (tm, 128), 1)
    sums = jnp.concatenate([jnp.where(lane128 < POOL_DIM, s2, s4), jnp.where(lane128 < POOL_DIM, s8, s16)],
                           axis=1)
    lane = lax.broadcasted_iota(jnp.int32, (tm, W_GROUP), 1)
    pos = i * tm + lax.broadcasted_iota(jnp.int32, (tm, W_GROUP), 0)
    half = jnp.left_shift(1, lane // POOL_DIM)
    cnt = jnp.minimum(pos + half, n_tok) - jnp.maximum(pos - half, 0)
    d = sums / cnt.astype(F32) - shifted(0, W_GROUP, W_GROUP)
    o_p = jnp.dot(d.astype(BF16), wp_ref[...], preferred_element_type=F32) * ps_ref[...]

    mixed = jnp.concatenate([o_f, att_ref[...], o_c, o_p], axis=1) * gates
    out = jnp.dot(mixed.astype(BF16), wo_ref[...], preferred_element_type=F32)
    o_ref[...] = x_ref[...] + gate_ref[...] * out


def _out_proj(x, gate, zz, bc, tu, att, pr, pi, cc, sc, wf_b, conv_w, conv_b, wp_b, pool_scale, wo_b, *, tm):
    b, s, _ = x.shape
    per_batch = gate.shape[0] > 1
    mod_map = (lambda bb, i: (bb, 0, 0)) if per_batch else (lambda bb, i: (0, 0, 0))
    const2 = lambda bb, i: (0, 0)
    tok = lambda w: pl.BlockSpec((None, tm, w), lambda bb, i: (bb, i, 0))
    hb = tm // POOL_HALO
    n_hblk = s // POOL_HALO
    full = lambda a: pl.BlockSpec(a.shape, const2)
    cw8 = jnp.concatenate([conv_w, jnp.zeros((8 - conv_w.shape[0], W_GROUP), F32)], axis=0)
    return pl.pallas_call(
        functools.partial(_out_kernel, tm=tm, n_tok=s),
        grid=(b, s // tm),
        in_specs=[tok(D_MODEL),
                  pl.BlockSpec((None, 1, D_MODEL), mod_map),
                  tok(4 * W_GROUP), tok(W_GROUP), tok(2 * W_GROUP),
                  pl.BlockSpec((None, POOL_HALO, 2 * W_GROUP),
                               lambda bb, i: (bb, jnp.maximum(i * hb - 1, 0), 0)),
                  pl.BlockSpec((None, POOL_HALO, 2 * W_GROUP),
                               lambda bb, i: (bb, jnp.minimum((i + 1) * hb, n_hblk - 1), 0)),
                  tok(W_GROUP), tok(W_GROUP), tok(W_GROUP),
                  full(cc), full(sc), full(wf_b), full(cw8),
                  pl.BlockSpec((1, W_GROUP), const2), full(wp_b),
                  pl.BlockSpec((1, W_GROUP), const2), full(wo_b)],
        out_specs=tok(D_MODEL),
        out_shape=jax.ShapeDtypeStruct((b, s, D_MODEL), F32),
        scratch_shapes=[pltpu.VMEM((tm + 2 * POOL_HALO, 2 * W_GROUP), F32)],
        compiler_params=_cparams("arbitrary", "arbitrary"),
        name="out_proj",
    )(x, gate, zz, bc, tu, tu, tu, att, pr, pi, cc, sc, wf_b, cw8, conv_b[None, :], wp_b,
      pool_scale[None, :], wo_b)


def _block_diag(blocks):
    n = blocks.shape[0]
    rows = [jnp.concatenate([blocks[i] if j == i else jnp.zeros_like(blocks[i]) for j in range(n)], axis=1)
            for i in range(n)]
    return jnp.concatenate(rows, axis=0)


def kernel(x, c, ctx, c_ctx, w_mod, b_mod, norm_g, w_in, q_gain, k_gain, w_fourier, conv_w, conv_b, pool_w,
           pool_scale, w_out):
    depth = w_mod.shape[0]
    b, s, _ = x.shape
    n_ctx = ctx.shape[1]
    mod = _modulation(c, c_ctx, w_mod, b_mod)
    cos_t, sin_t = _rope_tables(s)
    gmat = _group_mean_matrix()
    fft_tables = _fft_tables()
    cc_lat, sc_lat = _channel_dft(s)
    cc_ctx, sc_ctx = _channel_dft(n_ctx)

    xc = ctx
    for l in range(depth):
        ctx_out = l < depth - 1
        shift, scale, gate = (mod[l, :b, j * D_MODEL:(j + 1) * D_MODEL][:, None, :] for j in range(3))
        shift_c, scale_c, gate_c = (mod[l, b:b + 1, j * D_MODEL:(j + 1) * D_MODEL][:, None, :] for j in range(3))
        w_in_b = w_in[l].astype(BF16)
        wo_b = w_out[l].astype(BF16)
        wf_b = w_fourier[l].astype(BF16)
        wp_b = _block_diag(pool_w[l]).astype(BF16)

        q, k, v, uf, zz, bc, tu = _in_proj(x, shift, scale, norm_g[l], w_in_b, q_gain[l], k_gain[l], gmat,
                                           cos_t, sin_t, use_rope=True, tm=512)
        qc, kc, vc, ufc, zzc, bcc, tuc = _in_proj(xc, shift_c, scale_c, norm_g[l], w_in_b, q_gain[l], k_gain[l],
                                                  gmat, cos_t[:n_ctx], sin_t[:n_ctx], use_rope=False, tm=n_ctx)
        k_all = jnp.concatenate([kc, k], axis=1)
        v_all = jnp.concatenate([vc, v], axis=1)
        att = _attention(q, k_all, v_all, tq=256, tk=768)
        pr, pi = _position_dft(uf, fft_tables)
        x_new = _out_proj(x, gate, zz, bc, tu, att, pr, pi, cc_lat, sc_lat, wf_b, conv_w[l], conv_b[l], wp_b,
                          pool_scale[l], wo_b, tm=256)
        if ctx_out:
            att_c = _attention(qc, kc, vc, tq=n_ctx, tk=n_ctx)
            prc, pic = _ctx_position_dft(ufc)
            xc = _out_proj(xc, gate_c, zzc, bcc, tuc, att_c, prc, pic, cc_ctx, sc_ctx, wf_b, conv_w[l], conv_b[l],
                           wp_b, pool_scale[l], wo_b, tm=n_ctx)
        x = x_new
    return x
```

```python
import functools

import numpy as np
import jax
import jax.numpy as jnp
from jax import lax
from jax.experimental import pallas as pl
from jax.experimental.pallas import tpu as pltpu

D_MODEL = 1024
GRID_W = 64
W_GROUP = 256
HEAD_DIM = 64
N_HEADS = 4
N_KV_HEADS = 2
KV_DIM = N_KV_HEADS * HEAD_DIM
ATT_SCALE = HEAD_DIM ** -0.5
Q_SCALE = ATT_SCALE * float(np.log2(np.e))
ROPE_FREQS = HEAD_DIM // 4
ROPE_THETA = 10000.0
F_DIM = 64
POOL_WINDOWS = (2, 4, 8, 16)
POOL_DIM = 64
POOL_HALO = 8
EPS = 1e-6
D_IN = 2816
C_Q, C_K, C_V, C_ZATT, C_UF, C_ZF, C_BC, C_CC, C_HC, C_ZC, C_UP, C_ZP = (
    0, 256, 384, 512, 768, 1024, 1280, 1536, 1792, 2048, 2304, 2560)

FFT_N1 = 128
FFT_N2 = 64

VMEM_LIMIT = 56 * 1024 * 1024
MXU_DIM = 256
SOFTMAX_ROWS = 32
F32 = jnp.float32
BF16 = jnp.bfloat16
HI = lax.Precision.HIGHEST


def _silu(x):
    return x / (1.0 + jnp.exp(-x))


def _cparams(*sem):
    return pltpu.CompilerParams(dimension_semantics=sem, vmem_limit_bytes=VMEM_LIMIT)


def _rope_tables(n):
    pos = np.arange(n)
    row = (pos // GRID_W).astype(np.float64)
    col = (pos % GRID_W).astype(np.float64)
    inv = ROPE_THETA ** (-np.arange(ROPE_FREQS, dtype=np.float64) / ROPE_FREQS)
    ar, ac = row[:, None] * inv, col[:, None] * inv
    cos = np.concatenate([np.cos(ar), np.cos(ar), np.cos(ac), np.cos(ac)], axis=1)
    sin = np.concatenate([-np.sin(ar), np.sin(ar), -np.sin(ac), np.sin(ac)], axis=1)
    return (jnp.asarray(np.tile(cos, (1, 2)), F32), jnp.asarray(np.tile(sin, (1, 2)), F32))


def _dft_cs(n):
    ang = 2.0 * np.pi * np.outer(np.arange(n), np.arange(n)) / n
    return np.cos(ang), np.sin(ang)


def _fft_tables():
    c1, s1 = _dft_cs(FFT_N1)
    f1 = np.concatenate([c1, -s1], axis=0)
    c2, s2 = _dft_cs(FFT_N2)
    f2 = np.block([[c2, s2], [-s2, c2]])
    n = FFT_N1 * FFT_N2
    ang = 2.0 * np.pi * np.outer(np.arange(FFT_N2), np.arange(FFT_N1)) / n
    twc = np.repeat(np.cos(ang)[:, :, None], 128, axis=2)
    tws = np.repeat(np.sin(ang)[:, :, None], 128, axis=2)
    return (jnp.asarray(f1, F32), jnp.asarray(f2, F32), jnp.asarray(twc, F32), jnp.asarray(tws, F32))


def _channel_dft(n_pos):
    c, s = _dft_cs(F_DIM)
    scale = 1.0 / np.sqrt(float(n_pos) * F_DIM)
    eye = np.eye(W_GROUP // F_DIM)
    return (jnp.asarray(np.kron(eye, c) * scale, F32), jnp.asarray(np.kron(eye, s) * scale, F32))


def _group_mean_matrix():
    g = np.kron(np.eye(N_HEADS), np.full((HEAD_DIM, HEAD_DIM), 1.0 / HEAD_DIM))
    return jnp.asarray(g, BF16)


def _mod_kernel(c_ref, w_ref, b_ref, o_ref):
    a = _silu(c_ref[...]).astype(BF16)
    o_ref[...] = jnp.dot(a, w_ref[...], preferred_element_type=F32) + b_ref[...]


def _modulation(c, c_ctx, w_mod, b_mod):
    depth = w_mod.shape[0]
    b = c.shape[0]
    rows = jnp.concatenate([c, c_ctx[None], jnp.zeros((8 - b - 1, D_MODEL), F32)], axis=0)
    tn = 1024
    return pl.pallas_call(
        _mod_kernel,
        grid=(depth, 3 * D_MODEL // tn),
        in_specs=[pl.BlockSpec((8, D_MODEL), lambda l, j: (0, 0)),
                  pl.BlockSpec((None, D_MODEL, tn), lambda l, j: (l, 0, j)),
                  pl.BlockSpec((None, 1, tn), lambda l, j: (l, 0, j))],
        out_specs=pl.BlockSpec((None, 8, tn), lambda l, j: (l, 0, j)),
        out_shape=jax.ShapeDtypeStruct((depth, 8, 3 * D_MODEL), F32),
        compiler_params=_cparams("arbitrary", "arbitrary"),
        name="modulation",
    )(rows, w_mod.astype(BF16), b_mod[:, None, :])


def _head_norm(t, gmat, gain):
    sq = t * t
    hi = sq.astype(BF16)
    lo = (sq - hi.astype(F32)).astype(BF16)
    ms = (jnp.dot(hi, gmat, preferred_element_type=F32) + jnp.dot(lo, gmat, preferred_element_type=F32))
    return t * lax.rsqrt(ms + EPS) * gain


def _rope128(t, cos, sin):
    lane = lax.broadcasted_iota(jnp.int32, t.shape, 1)
    first = (lane % 32) < ROPE_FREQS
    partner = jnp.where(first, pltpu.roll(t, 128 - ROPE_FREQS, axis=1), pltpu.roll(t, ROPE_FREQS, axis=1))
    return t * cos + partner * sin


def _in_kernel(x_ref, shift_ref, scale_ref, g_ref, w_ref, qg_ref, kg_ref, gm_ref, cos_ref, sin_ref,
               q_ref, k_ref, v_ref, uf_ref, zz_ref, bc_ref, tu_ref, *, use_rope):
    x = x_ref[...]
    ms = jnp.mean(x * x, axis=-1, keepdims=True)
    y = x * lax.rsqrt(ms + EPS) * g_ref[...]
    hb = (y * (1.0 + scale_ref[...]) + shift_ref[...]).astype(BF16)

    def part(start, width=W_GROUP):
        return jnp.dot(hb, w_ref[:, start:start + width], preferred_element_type=F32)

    gm = gm_ref[...]
    q = _head_norm(part(C_Q), gm, qg_ref[...])
    k = _head_norm(part(C_K, KV_DIM), gm[:KV_DIM, :KV_DIM], kg_ref[...])
    if use_rope:
        cos, sin = cos_ref[...], sin_ref[...]
        q = jnp.concatenate([_rope128(q[:, :128], cos, sin), _rope128(q[:, 128:], cos, sin)], axis=1)
        k = _rope128(k, cos, sin)
    q_ref[...] = (q * Q_SCALE).astype(BF16)
    k_ref[...] = k.astype(BF16)
    v_ref[...] = part(C_V, KV_DIM).astype(BF16)
    uf_ref[...] = part(C_UF)
    zz_ref[:, 0:256] = part(C_ZF)
    zz_ref[:, 256:512] = part(C_ZATT)
    zz_ref[:, 512:768] = part(C_ZC)
    zz_ref[:, 768:1024] = part(C_ZP)
    bc_ref[...] = part(C_BC)
    tu_ref[:, 0:256] = part(C_CC) * part(C_HC)
    tu_ref[:, 256:512] = part(C_UP)


def _in_proj(x, shift, scale, norm_g, w_in_b, q_gain, k_gain, gmat, cos_t, sin_t, *, use_rope, tm):
    b, s, _ = x.shape
    per_batch = shift.shape[0] > 1
    mod_map = (lambda bb, i: (bb, 0, 0)) if per_batch else (lambda bb, i: (0, 0, 0))
    const2 = lambda bb, i: (0, 0)
    tok = lambda w: pl.BlockSpec((None, tm, w), lambda bb, i: (bb, i, 0))
    out_dt = [(W_GROUP, BF16), (KV_DIM, BF16), (KV_DIM, BF16), (W_GROUP, F32), (4 * W_GROUP, F32),
              (W_GROUP, F32), (2 * W_GROUP, F32)]
    return pl.pallas_call(
        functools.partial(_in_kernel, use_rope=use_rope),
        grid=(b, s // tm),
        in_specs=[tok(D_MODEL),
                  pl.BlockSpec((None, 1, D_MODEL), mod_map),
                  pl.BlockSpec((None, 1, D_MODEL), mod_map),
                  pl.BlockSpec((1, D_MODEL), const2),
                  pl.BlockSpec((D_MODEL, D_IN), const2),
                  pl.BlockSpec((1, W_GROUP), const2),
                  pl.BlockSpec((1, KV_DIM), const2),
                  pl.BlockSpec((W_GROUP, W_GROUP), const2),
                  pl.BlockSpec((tm, 128), lambda bb, i: (i, 0)),
                  pl.BlockSpec((tm, 128), lambda bb, i: (i, 0))],
        out_specs=[tok(w) for w, _ in out_dt],
        out_shape=[jax.ShapeDtypeStruct((b, s, w), dt) for w, dt in out_dt],
        compiler_params=_cparams("arbitrary", "arbitrary"),
        name="in_proj_rope" if use_rope else "in_proj_ctx",
    )(x, shift, scale, norm_g[None, :], w_in_b, jnp.tile(q_gain, N_HEADS)[None, :],
      jnp.tile(k_gain, N_KV_HEADS)[None, :], gmat, cos_t, sin_t)


def _attn_kernel(q_ref, k_ref, v_ref, o_ref, qs_scr, s0, s1, p0, p1, mc0, mc1, al0, al1, m_scr, l_scr, acc_scr,
                 *, tq, tk, n_chunks):
    s_scr, p_scr, mc_scr, al_scr = (s0, s1), (p0, p1), (mc0, mc1), (al0, al1)
    g = pl.program_id(2)
    lane = lax.broadcasted_iota(jnp.int32, (tq, 128), 1)
    qb = q_ref[...].astype(F32)
    swapped = pltpu.roll(qb, HEAD_DIM, axis=1)
    keep = (lane // HEAD_DIM) == g
    head_a = jnp.where(g == 0, qb, swapped)
    head_b = jnp.where(g == 0, swapped, qb)
    qs_scr[...] = jnp.concatenate([jnp.where(keep, head_a, 0.0), jnp.where(keep, head_b, 0.0)],
                                  axis=0).astype(BF16)
    m_scr[...] = jnp.full(m_scr.shape, -jnp.inf, F32)
    l_scr[...] = jnp.zeros(l_scr.shape, F32)
    acc_scr[...] = jnp.zeros(acc_scr.shape, F32)

    def rows(c):
        return pl.ds(c * tk if isinstance(c, int) else pl.multiple_of(c * tk, tk), tk)

    def scores(c, slot):
        s = lax.dot_general(qs_scr[...], k_ref[rows(c), :], (((1,), (1,)), ((), ())),
                            preferred_element_type=F32)
        s_scr[slot][...] = s
        mc_scr[slot][...] = jnp.broadcast_to(jnp.max(s, axis=1, keepdims=True), mc_scr[slot].shape)

    def softmax(slot):
        for r in range(2 * tq // SOFTMAX_ROWS):
            rs = slice(r * SOFTMAX_ROWS, (r + 1) * SOFTMAX_ROWS)
            m_prev = m_scr[rs, :]
            m_new = jnp.maximum(m_prev, mc_scr[slot][rs, :])
            alpha = jnp.exp2(m_prev - m_new)
            lsum = alpha * l_scr[rs, :]
            for j in range(tk // 128):
                cols = slice(j * 128, (j + 1) * 128)
                p = jnp.exp2(s_scr[slot][rs, cols] - m_new)
                lsum = lsum + p
                p_scr[slot][rs, cols] = p.astype(BF16)
            l_scr[rs, :] = lsum
            al_scr[slot][rs, :] = alpha
            m_scr[rs, :] = m_new

    def pv(c, slot):
        acc_scr[...] = al_scr[slot][...] * acc_scr[...] + jnp.dot(p_scr[slot][...], v_ref[rows(c), :],
                                                                  preferred_element_type=F32)

    def step(c, slot):
        scores(c + 1, 1 - slot)
        softmax(slot)
        pv(c - 1, 1 - slot)

    scores(0, 0)
    if n_chunks > 1:
        scores(1, 1)
    softmax(0)
    n_steady = max(n_chunks - 2, 0)

    def pair(i, carry):
        c = 1 + 2 * i
        step(c, 1)
        step(c + 1, 0)
        return carry

    lax.fori_loop(0, n_steady // 2, pair, 0)
    if n_steady % 2:
        step(n_chunks - 2, (n_chunks - 2) % 2)
    if n_chunks > 1:
        softmax((n_chunks - 1) % 2)
        pv(n_chunks - 2, (n_chunks - 2) % 2)
    pv(n_chunks - 1, (n_chunks - 1) % 2)

    o = acc_scr[...] / jnp.sum(l_scr[...], axis=1, keepdims=True)
    oa, ob = o[:tq], o[tq:]
    oa_sw, ob_sw = pltpu.roll(oa, HEAD_DIM, axis=1), pltpu.roll(ob, HEAD_DIM, axis=1)
    left = jnp.where(g == 0, oa, oa_sw)
    right = jnp.where(g == 0, ob_sw, ob)
    o_ref[...] = jnp.where(lane < HEAD_DIM, left, right)


def _attention(q, k, v, *, tq, tk):
    b, sq, _ = q.shape
    skv = k.shape[1]
    kern = functools.partial(_attn_kernel, tq=tq, tk=tk, n_chunks=skv // tk)
    rows2 = 2 * tq
    return pl.pallas_call(
        kern,
        grid=(b, sq // tq, N_KV_HEADS),
        in_specs=[pl.BlockSpec((None, tq, 128), lambda bb, i, g: (bb, i, g)),
                  pl.BlockSpec((None, skv, KV_DIM), lambda bb, i, g: (bb, 0, 0)),
                  pl.BlockSpec((None, skv, KV_DIM), lambda bb, i, g: (bb, 0, 0))],
        out_specs=pl.BlockSpec((None, tq, 128), lambda bb, i, g: (bb, i, g)),
        out_shape=jax.ShapeDtypeStruct((b, sq, W_GROUP), F32),
        scratch_shapes=[pltpu.VMEM((rows2, 128), BF16),
                        pltpu.VMEM((rows2, tk), F32), pltpu.VMEM((rows2, tk), F32),
                        pltpu.VMEM((rows2, tk), BF16), pltpu.VMEM((rows2, tk), BF16),
                        pltpu.VMEM((rows2, 128), F32), pltpu.VMEM((rows2, 128), F32),
                        pltpu.VMEM((rows2, 128), F32), pltpu.VMEM((rows2, 128), F32),
                        pltpu.VMEM((rows2, 128), F32),
                        pltpu.VMEM((rows2, 128), F32),
                        pltpu.VMEM((rows2, 128), F32)],
        compiler_params=_cparams("arbitrary", "arbitrary", "arbitrary"),
        name="attention",
    )(q, k, v)


def _fft1_kernel(x_ref, f1_ref, twc_ref, tws_ref, tr_ref, ti_ref, *, nb):
    t = jnp.dot(f1_ref[...], x_ref[...], precision=HI, preferred_element_type=F32)
    for j in range(nb):
        tr = t[:FFT_N1, j * W_GROUP:(j + 1) * W_GROUP]
        ti = t[FFT_N1:, j * W_GROUP:(j + 1) * W_GROUP]
        c = jnp.concatenate([twc_ref[j], twc_ref[j]], axis=1)
        s = jnp.concatenate([tws_ref[j], tws_ref[j]], axis=1)
        tr_ref[j] = tr * c + ti * s
        ti_ref[j] = ti * c - tr * s


def _fft2_kernel(tr_ref, ti_ref, f2_ref, pr_ref, pi_ref):
    xin = jnp.concatenate([tr_ref[...], ti_ref[...]], axis=0)
    o = jnp.dot(f2_ref[...], xin, precision=HI, preferred_element_type=F32)
    pr_ref[...] = o[:FFT_N2]
    pi_ref[...] = o[FFT_N2:]


def _position_dft(u, tables):
    f1, f2, twc, tws = tables
    b, n, w = u.shape
    assert n == FFT_N1 * FFT_N2 and w == W_GROUP
    nb = 8
    x2d = u.reshape(b, FFT_N1, FFT_N2 * w)
    t_shape = jax.ShapeDtypeStruct((b, FFT_N2, FFT_N1, w), F32)
    tr, ti = pl.pallas_call(
        functools.partial(_fft1_kernel, nb=nb),
        grid=(FFT_N2 // nb, b),
        in_specs=[pl.BlockSpec((None, FFT_N1, nb * w), lambda j, bb: (bb, 0, j)),
                  pl.BlockSpec((2 * FFT_N1, FFT_N1), lambda j, bb: (0, 0)),
                  pl.BlockSpec((nb, FFT_N1, 128), lambda j, bb: (j, 0, 0)),
                  pl.BlockSpec((nb, FFT_N1, 128), lambda j, bb: (j, 0, 0))],
        out_specs=[pl.BlockSpec((None, nb, FFT_N1, w), lambda j, bb: (bb, j, 0, 0))] * 2,
        out_shape=[t_shape, t_shape],
        compiler_params=_cparams("arbitrary", "arbitrary"),
        name="fft_stage1",
    )(x2d, f1, twc, tws)
    ln = 4096
    width = FFT_N1 * w
    o_shape = jax.ShapeDtypeStruct((b, FFT_N2, width), F32)
    pr, pi = pl.pallas_call(
        _fft2_kernel,
        grid=(b, width // ln),
        in_specs=[pl.BlockSpec((None, FFT_N2, ln), lambda bb, j: (bb, 0, j)),
                  pl.BlockSpec((None, FFT_N2, ln), lambda bb, j: (bb, 0, j)),
                  pl.BlockSpec((2 * FFT_N2, 2 * FFT_N2), lambda bb, j: (0, 0))],
        out_specs=[pl.BlockSpec((None, FFT_N2, ln), lambda bb, j: (bb, 0, j))] * 2,
        out_shape=[o_shape, o_shape],
        compiler_params=_cparams("arbitrary", "arbitrary"),
        name="fft_stage2",
    )(tr.reshape(b, FFT_N2, width), ti.reshape(b, FFT_N2, width), f2)
    return pr.reshape(b, n, w), pi.reshape(b, n, w)


def _ctx_dft_kernel(u_ref, f_ref, pr_ref, pi_ref):
    n = u_ref.shape[0]
    o = jnp.dot(f_ref[...], u_ref[...], precision=HI, preferred_element_type=F32)
    pr_ref[...] = o[:n]
    pi_ref[...] = o[n:]


def _ctx_position_dft(u):
    b, n, w = u.shape
    c, s = _dft_cs(n)
    f = jnp.asarray(np.concatenate([c, -s], axis=0), F32)
    shape = jax.ShapeDtypeStruct((b, n, w), F32)
    spec = pl.BlockSpec((None, n, w), lambda bb: (bb, 0, 0))
    return pl.pallas_call(
        _ctx_dft_kernel,
        grid=(b,),
        in_specs=[spec, pl.BlockSpec((2 * n, n), lambda bb: (0, 0))],
        out_specs=[spec, spec],
        out_shape=[shape, shape],
        compiler_params=_cparams("arbitrary"),
        name="ctx_dft",
    )(u, f)


def _out_kernel(x_ref, gate_ref, zz_ref, bc_ref, tu_ref, tu_prev_ref, tu_next_ref, att_ref, pr_ref, pi_ref,
                cc_ref, sc_ref, wf_ref, cw_ref, cb_ref, wp_ref, ps_ref, wo_ref, o_ref, ext_scr, *, tm, n_tok):
    i = pl.program_id(1)
    last = pl.num_programs(1) - 1
    h = POOL_HALO
    ext_scr[0:h, :] = jnp.where(i > 0, tu_prev_ref[...], 0.0)
    ext_scr[h:h + tm, :] = tu_ref[...]
    ext_scr[h + tm:, :] = jnp.where(i < last, tu_next_ref[...], 0.0)

    def shifted(d, lo, width):
        return ext_scr[pl.ds(h + d, tm), lo:lo + width]

    gates = _silu(zz_ref[...])

    y = (jnp.dot(pr_ref[...], cc_ref[...], precision=HI, preferred_element_type=F32)
         + jnp.dot(pi_ref[...], sc_ref[...], precision=HI, preferred_element_type=F32))
    o_f = jnp.dot(y.astype(BF16), wf_ref[...], preferred_element_type=F32)

    cw = cw_ref[...]
    yc = (shifted(-1, 0, W_GROUP) * cw[0:1] + shifted(0, 0, W_GROUP) * cw[1:2]
          + shifted(1, 0, W_GROUP) * cw[2:3] + cb_ref[...])
    o_c = bc_ref[...] * yc

    def wsum(lo, d_from, d_to):
        acc = shifted(d_from, lo, 128)
        for d in range(d_from + 1, d_to + 1):
            acc = acc + shifted(d, lo, 128)
        return acc

    s2 = wsum(W_GROUP, -1, 0)
    s4 = s2 + shifted(-2, W_GROUP, 128) + shifted(1, W_GROUP, 128)
    s8 = wsum(W_GROUP + 128, -4, 3)
    s16 = s8 + wsum(W_GROUP + 128, -8, -5) + wsum(W_GROUP + 128, 4, 7)
    lane128 = lax.broadcasted_iota(jnp.int32, (tm, 128), 1)
    sums = jnp.concatenate([jnp.where(lane128 < POOL_DIM, s2, s4), jnp.where(lane128 < POOL_DIM, s8, s16)],
                           axis=1)
    lane = lax.broadcasted_iota(jnp.int32, (tm, W_GROUP), 1)
    pos = i * tm + lax.broadcasted_iota(jnp.int32, (tm, W_GROUP), 0)
    half = jnp.left_shift(1, lane // POOL_DIM)
    cnt = jnp.minimum(pos + half, n_tok) - jnp.maximum(pos - half, 0)
    d = sums / cnt.astype(F32) - shifted(0, W_GROUP, W_GROUP)
    o_p = jnp.dot(d.astype(BF16), wp_ref[...], preferred_element_type=F32) * ps_ref[...]

    mixed = jnp.concatenate([o_f, att_ref[...], o_c, o_p], axis=1) * gates
    out = jnp.dot(mixed.astype(BF16), wo_ref[...], preferred_element_type=F32)
    o_ref[...] = x_ref[...] + gate_ref[...] * out


def _out_proj(x, gate, zz, bc, tu, att, pr, pi, cc, sc, wf_b, conv_w, conv_b, wp_b, pool_scale, wo_b, *, tm):
    b, s, _ = x.shape
    per_batch = gate.shape[0] > 1
    mod_map = (lambda bb, i: (bb, 0, 0)) if per_batch else (lambda bb, i: (0, 0, 0))
    const2 = lambda bb, i: (0, 0)
    tok = lambda w: pl.BlockSpec((None, tm, w), lambda bb, i: (bb, i, 0))
    hb = tm // POOL_HALO
    n_hblk = s // POOL_HALO
    full = lambda a: pl.BlockSpec(a.shape, const2)
    cw8 = jnp.concatenate([conv_w, jnp.zeros((8 - conv_w.shape[0], W_GROUP), F32)], axis=0)
    return pl.pallas_call(
        functools.partial(_out_kernel, tm=tm, n_tok=s),
        grid=(b, s // tm),
        in_specs=[tok(D_MODEL),
                  pl.BlockSpec((None, 1, D_MODEL), mod_map),
                  tok(4 * W_GROUP), tok(W_GROUP), tok(2 * W_GROUP),
                  pl.BlockSpec((None, POOL_HALO, 2 * W_GROUP),
                               lambda bb, i: (bb, jnp.maximum(i * hb - 1, 0), 0)),
                  pl.BlockSpec((None, POOL_HALO, 2 * W_GROUP),
                               lambda bb, i: (bb, jnp.minimum((i + 1) * hb, n_hblk - 1), 0)),
                  tok(W_GROUP), tok(W_GROUP), tok(W_GROUP),
                  full(cc), full(sc), full(wf_b), full(cw8),
                  pl.BlockSpec((1, W_GROUP), const2), full(wp_b),
                  pl.BlockSpec((1, W_GROUP), const2), full(wo_b)],
        out_specs=tok(D_MODEL),
        out_shape=jax.ShapeDtypeStruct((b, s, D_MODEL), F32),
        scratch_shapes=[pltpu.VMEM((tm + 2 * POOL_HALO, 2 * W_GROUP), F32)],
        compiler_params=_cparams("arbitrary", "arbitrary"),
        name="out_proj",
    )(x, gate, zz, bc, tu, tu, tu, att, pr, pi, cc, sc, wf_b, cw8, conv_b[None, :], wp_b,
      pool_scale[None, :], wo_b)


def _block_diag(blocks):
    n = blocks.shape[0]
    rows = [jnp.concatenate([blocks[i] if j == i else jnp.zeros_like(blocks[i]) for j in range(n)], axis=1)
            for i in range(n)]
    return jnp.concatenate(rows, axis=0)


def kernel(x, c, ctx, c_ctx, w_mod, b_mod, norm_g, w_in, q_gain, k_gain, w_fourier, conv_w, conv_b, pool_w,
           pool_scale, w_out):
    depth = w_mod.shape[0]
    b, s, _ = x.shape
    n_ctx = ctx.shape[1]
    mod = _modulation(c, c_ctx, w_mod, b_mod)
    cos_t, sin_t = _rope_tables(s)
    gmat = _group_mean_matrix()
    fft_tables = _fft_tables()
    cc_lat, sc_lat = _channel_dft(s)
    cc_ctx, sc_ctx = _channel_dft(n_ctx)

    xc = ctx
    for l in range(depth):
        ctx_out = l < depth - 1
        shift, scale, gate = (mod[l, :b, j * D_MODEL:(j + 1) * D_MODEL][:, None, :] for j in range(3))
        shift_c, scale_c, gate_c = (mod[l, b:b + 1, j * D_MODEL:(j + 1) * D_MODEL][:, None, :] for j in range(3))
        w_in_b = w_in[l].astype(BF16)
        wo_b = w_out[l].astype(BF16)
        wf_b = w_fourier[l].astype(BF16)
        wp_b = _block_diag(pool_w[l]).astype(BF16)

        q, k, v, uf, zz, bc, tu = _in_proj(x, shift, scale, norm_g[l], w_in_b, q_gain[l], k_gain[l], gmat,
                                           cos_t, sin_t, use_rope=True, tm=512)
        qc, kc, vc, ufc, zzc, bcc, tuc = _in_proj(xc, shift_c, scale_c, norm_g[l], w_in_b, q_gain[l], k_gain[l],
                                                  gmat, cos_t[:n_ctx], sin_t[:n_ctx], use_rope=False, tm=n_ctx)
        k_all = jnp.concatenate([kc, k], axis=1)
        v_all = jnp.concatenate([vc, v], axis=1)
        att = _attention(q, k_all, v_all, tq=256, tk=768)
        pr, pi = _position_dft(uf, fft_tables)
        x_new = _out_proj(x, gate, zz, bc, tu, att, pr, pi, cc_lat, sc_lat, wf_b, conv_w[l], conv_b[l], wp_b,
                          pool_scale[l], wo_b, tm=256)
        if ctx_out:
            att_c = _attention(qc, kc, vc, tq=n_ctx, tk=n_ctx)
            prc, pic = _ctx_position_dft(ufc)
            xc = _out_proj(xc, gate_c, zzc, bcc, tuc, att_c, prc, pic, cc_ctx, sc_ctx, wf_b, conv_w[l], conv_b[l],
                           wp_b, pool_scale[l], wo_b, tm=n_ctx)
        x = x_new
    return x
```
